```python
import math
import jax, jax.numpy as jnp
from jax import lax
import numpy as np

D_MODEL = 1024
BATCH = 2
SEQ = 8192
DEPTH = 1
DEC_BATCH = 16
DEC_SEQ = 16
PAST_LEN = 2048

CHUNK = 64
D_SSM = 512
SSM_GROUP = 16
N_SSM_GROUPS = D_SSM // SSM_GROUP
SSM_STATE = 64
D_POOL = 256
POOL_WINDOWS = (2, 4, 8, 16)
N_POOL_GROUPS = len(POOL_WINDOWS)
POOL_GROUP = D_POOL // N_POOL_GROUPS
POOL_BUF = max(POOL_WINDOWS) - 1
N_MEM = 256
MEM_HEADS = 4
D_MEM = 256
MEM_HEAD_DIM = D_MEM // MEM_HEADS
D_MIX = D_SSM + D_POOL + D_MEM
N_BRANCH = 3
D_FF = 2816
CONV_W = 3
EPS = 1e-6
DT_MIN = 1e-3
DT_MAX = 1e-1

kernel_name = "hybrid_streaming_encoder_step"

F32 = jnp.float32


def rmsnorm(x, g):
    xf = x.astype(F32)
    y = xf * lax.rsqrt(jnp.mean(xf * xf, axis=-1, keepdims=True) + EPS)
    return (y * g.astype(F32)).astype(x.dtype)


def s5_scan(u, h0_re, h0_im, lam_re, lam_im, log_dt, b_re, b_im, c_re, c_im, d_skip):
    bsz, L, _ = u.shape
    uf = u.astype(F32)
    ug = uf.reshape(bsz, L, N_SSM_GROUPS, SSM_GROUP)
    lr = lam_re.astype(F32)
    li = lam_im.astype(F32)
    dt = jnp.exp(log_dt.astype(F32))[:, None]
    mag = jnp.exp(lr * dt)
    ab_re = mag * jnp.cos(li * dt)
    ab_im = mag * jnp.sin(li * dt)
    den = lr * lr + li * li
    nr = ab_re - 1.0
    ni = ab_im
    z_re = (nr * lr + ni * li) / den
    z_im = (ni * lr - nr * li) / den
    br = b_re.astype(F32)
    bi = b_im.astype(F32)
    bb_re = z_re[..., None] * br - z_im[..., None] * bi
    bb_im = z_re[..., None] * bi + z_im[..., None] * br
    bu_re = jnp.einsum('gnh,blgh->blgn', bb_re, ug)
    bu_im = jnp.einsum('gnh,blgh->blgn', bb_im, ug)
    h0r = h0_re.astype(F32)
    h0i = h0_im.astype(F32)
    bu_re = bu_re.at[:, 0].add(ab_re * h0r - ab_im * h0i)
    bu_im = bu_im.at[:, 0].add(ab_re * h0i + ab_im * h0r)
    a_re = jnp.broadcast_to(ab_re, bu_re.shape)
    a_im = jnp.broadcast_to(ab_im, bu_im.shape)

    def combine(e1, e2):
        a1r, a1i, b1r, b1i = e1
        a2r, a2i, b2r, b2i = e2
        return (a2r * a1r - a2i * a1i,
                a2r * a1i + a2i * a1r,
                a2r * b1r - a2i * b1i + b2r,
                a2r * b1i + a2i * b1r + b2i)

    _, _, hr, hi = lax.associative_scan(combine, (a_re, a_im, bu_re, bu_im), axis=1)
    y = (jnp.einsum('ghn,blgn->blgh', c_re.astype(F32), hr)
         - jnp.einsum('ghn,blgn->blgh', c_im.astype(F32), hi))
    y = y.reshape(bsz, L, D_SSM) + d_skip.astype(F32) * uf
    return y.astype(u.dtype), hr[:, -1].astype(h0_re.dtype), hi[:, -1].astype(h0_im.dtype)


def multiscale_pool(u, buf, pos0, w_pool, pool_scale):
    bsz, L, _ = u.shape
    ext = jnp.concatenate([buf.astype(u.dtype), u], axis=1)
    cs = jnp.cumsum(ext.astype(F32), axis=1)
    cs = jnp.concatenate([jnp.zeros_like(cs[:, :1]), cs], axis=1)
    end = cs[:, POOL_BUF + 1:POOL_BUF + 1 + L]
    pos = pos0 + jnp.arange(L)
    means = []
    for gi, w in enumerate(POOL_WINDOWS):
        sl = slice(gi * POOL_GROUP, (gi + 1) * POOL_GROUP)
        start = cs[:, POOL_BUF + 1 - w:POOL_BUF + 1 - w + L, sl]
        cnt = jnp.minimum(pos + 1, w).astype(F32)[None, :, None]
        means.append((end[..., sl] - start) / cnt)
    diff = jnp.concatenate(means, axis=-1) - u.astype(F32)
    diff = diff.astype(u.dtype).reshape(bsz, L, N_POOL_GROUPS, POOL_GROUP)
    y = jnp.einsum('blgc,gcd->blgd', diff, w_pool).reshape(bsz, L, D_POOL) * pool_scale
    return y, ext[:, -POOL_BUF:]


def memory_kv(mem, g_mem, w_mem_k, w_mem_v):
    bsz = mem.shape[0]
    mn = rmsnorm(mem, g_mem)
    k = (mn @ w_mem_k).reshape(bsz, N_MEM, MEM_HEADS, MEM_HEAD_DIM)
    v = (mn @ w_mem_v).reshape(bsz, N_MEM, MEM_HEADS, MEM_HEAD_DIM)
    return k, v


def memory_attention(q, mk, mv):
    bsz, L, _ = q.shape
    qh = q.reshape(bsz, L, MEM_HEADS, MEM_HEAD_DIM)
    s = jnp.einsum('blhd,bmhd->bhlm', qh.astype(F32), mk.astype(F32)) * (MEM_HEAD_DIM ** -0.5)
    p = jax.nn.softmax(s, axis=-1).astype(mv.dtype)
    o = jnp.einsum('bhlm,bmhd->blhd', p, mv)
    return o.reshape(bsz, L, D_MEM).astype(q.dtype)


def conv_ffn(h, buf, w_ffn_up, w_dw, b_dw, w_ffn_down):
    L = h.shape[1]
    up = h @ w_ffn_up
    a, g = jnp.split(up, 2, axis=-1)
    ext = jnp.concatenate([buf.astype(a.dtype), a], axis=1)
    conv = ext[:, 0:L] * w_dw[0]
    for k in range(1, CONV_W):
        conv = conv + ext[:, k:k + L] * w_dw[k]
    conv = conv + b_dw
    act = jax.nn.gelu(conv) * g
    return act @ w_ffn_down, ext[:, -(CONV_W - 1):]


def layer(x, mk, mv, ssm_re, ssm_im, pool_buf, conv_buf, pos0, p):
    bsz, L, _ = x.shape
    h = rmsnorm(x, p['g_pre1'])
    z = h @ p['w_in']
    u_ssm = z[..., :D_SSM]
    u_pool = z[..., D_SSM:D_SSM + D_POOL]
    q_mem = z[..., D_SSM + D_POOL:]
    y_ssm, ssm_re_new, ssm_im_new = s5_scan(u_ssm, ssm_re, ssm_im, p['ssm_lam_re'], p['ssm_lam_im'],
                                            p['ssm_log_dt'], p['ssm_b_re'], p['ssm_b_im'],
                                            p['ssm_c_re'], p['ssm_c_im'], p['ssm_d'])
    y_ssm = jax.nn.gelu(y_ssm)
    y_ssm = y_ssm * jax.nn.sigmoid(y_ssm @ p['w_glu'] + p['b_glu'])
    y_pool, pool_new = multiscale_pool(u_pool, pool_buf, pos0, p['w_pool'], p['pool_scale'])
    y_mem = memory_attention(q_mem, mk, mv)
    gates = jax.nn.sigmoid(h @ p['w_gate'] + p['b_gate']).reshape(bsz, L, N_BRANCH, D_MODEL)
    merged = (gates[:, :, 0] * (y_ssm @ p['w_br_ssm'])
              + gates[:, :, 1] * (y_pool @ p['w_br_pool'])
              + gates[:, :, 2] * (y_mem @ p['w_br_mem']))
    x = x + rmsnorm(merged @ p['w_out'], p['g_post1'])
    h2 = rmsnorm(x, p['g_pre2'])
    f, conv_new = conv_ffn(h2, conv_buf, p['w_ffn_up'], p['w_dw'], p['b_dw'], p['w_ffn_down'])
    x = x + rmsnorm(f, p['g_post2'])
    return x, ssm_re_new, ssm_im_new, pool_new, conv_new


def setup_inputs(seed: int = 0) -> dict:
    key = jax.random.key(seed)
    ks = iter(jax.random.split(key, 48))

    def nrm(shape, scale=1.0):
        return jax.random.normal(next(ks), shape, F32) * scale

    def gain(shape):
        return 1.0 + nrm(shape, 0.02)

    G, N, H = N_SSM_GROUPS, SSM_STATE, SSM_GROUP
    lam_im0 = jnp.pi * jnp.arange(N, dtype=F32)
    inp = {
        'x_prompt': nrm((BATCH, SEQ, D_MODEL)),
        'x_sample': nrm((DEC_BATCH, DEC_SEQ, D_MODEL)),
        'mem_prompt': nrm((BATCH, N_MEM, D_MODEL)),
        'cache_mem_k': nrm((DEPTH, DEC_BATCH, N_MEM, MEM_HEADS, MEM_HEAD_DIM)),
        'cache_mem_v': nrm((DEPTH, DEC_BATCH, N_MEM, MEM_HEADS, MEM_HEAD_DIM)),
        'state_ssm_re': nrm((DEPTH, DEC_BATCH, G, N), 0.5),
        'state_ssm_im': nrm((DEPTH, DEC_BATCH, G, N), 0.5),
        'state_pool': nrm((DEPTH, DEC_BATCH, POOL_BUF, D_POOL)),
        'state_conv': nrm((DEPTH, DEC_BATCH, CONV_W - 1, D_FF)),
        'g_pre1': gain((DEPTH, D_MODEL)),
        'w_in': nrm((DEPTH, D_MODEL, D_MIX), D_MODEL ** -0.5),
        'ssm_lam_re': -0.5 + nrm((DEPTH, G, N), 0.01),
        'ssm_lam_im': lam_im0 + nrm((DEPTH, G, N), 0.01),
        'ssm_log_dt': jax.random.uniform(next(ks), (DEPTH, G), F32, math.log(DT_MIN), math.log(DT_MAX)),
        'ssm_b_re': nrm((DEPTH, G, N, H), (2.0 * H) ** -0.5),
        'ssm_b_im': nrm((DEPTH, G, N, H), (2.0 * H) ** -0.5),
        'ssm_c_re': nrm((DEPTH, G, H, N), (2.0 * N) ** -0.5),
        'ssm_c_im': nrm((DEPTH, G, H, N), (2.0 * N) ** -0.5),
        'ssm_d': nrm((DEPTH, D_SSM)),
        'w_glu': nrm((DEPTH, D_SSM, D_SSM), D_SSM ** -0.5),
        'b_glu': nrm((DEPTH, D_SSM), 0.02),
        'w_pool': nrm((DEPTH, N_POOL_GROUPS, POOL_GROUP, POOL_GROUP), POOL_GROUP ** -0.5),
        'pool_scale': gain((DEPTH, D_POOL)),
        'g_mem': gain((DEPTH, D_MODEL)),
        'w_mem_k': nrm((DEPTH, D_MODEL, D_MEM), D_MODEL ** -0.5),
        'w_mem_v': nrm((DEPTH, D_MODEL, D_MEM), D_MODEL ** -0.5),
        'w_gate': nrm((DEPTH, D_MODEL, N_BRANCH * D_MODEL), D_MODEL ** -0.5),
        'b_gate': nrm((DEPTH, N_BRANCH * D_MODEL), 0.02),
        'w_br_ssm': nrm((DEPTH, D_SSM, D_MODEL), D_SSM ** -0.5),
        'w_br_pool': nrm((DEPTH, D_POOL, D_MODEL), D_POOL ** -0.5),
        'w_br_mem': nrm((DEPTH, D_MEM, D_MODEL), D_MEM ** -0.5),
        'w_out': nrm((DEPTH, D_MODEL, D_MODEL), D_MODEL ** -0.5),
        'g_post1': gain((DEPTH, D_MODEL)),
        'g_pre2': gain((DEPTH, D_MODEL)),
        'w_ffn_up': nrm((DEPTH, D_MODEL, 2 * D_FF), D_MODEL ** -0.5),
        'w_dw': nrm((DEPTH, CONV_W, D_FF), CONV_W ** -0.5),
        'b_dw': nrm((DEPTH, D_FF), 0.02),
        'w_ffn_down': nrm((DEPTH, D_FF, D_MODEL), D_FF ** -0.5),
        'g_post2': gain((DEPTH, D_MODEL)),
    }
    return inp


def reference(x_prompt, x_sample, mem_prompt, cache_mem_k, cache_mem_v, state_ssm_re, state_ssm_im,
              state_pool, state_conv, g_pre1, w_in, ssm_lam_re, ssm_lam_im, ssm_log_dt, ssm_b_re,
              ssm_b_im, ssm_c_re, ssm_c_im, ssm_d, w_glu, b_glu, w_pool, pool_scale, g_mem, w_mem_k,
              w_mem_v, w_gate, b_gate, w_br_ssm, w_br_pool, w_br_mem, w_out, g_post1, g_pre2,
              w_ffn_up, w_dw, b_dw, w_ffn_down, g_post2):
    xp = x_prompt
    xs = x_sample
    mk_l, mv_l = [], []
    srp_l, sip_l, srs_l, sis_l = [], [], [], []
    pp_l, ps_l, cp_l, cs_l = [], [], [], []
    dt = x_prompt.dtype
    for l in range(DEPTH):
        p = dict(g_pre1=g_pre1[l], w_in=w_in[l], ssm_lam_re=ssm_lam_re[l], ssm_lam_im=ssm_lam_im[l],
                 ssm_log_dt=ssm_log_dt[l], ssm_b_re=ssm_b_re[l], ssm_b_im=ssm_b_im[l],
                 ssm_c_re=ssm_c_re[l], ssm_c_im=ssm_c_im[l], ssm_d=ssm_d[l], w_glu=w_glu[l],
                 b_glu=b_glu[l], w_pool=w_pool[l], pool_scale=pool_scale[l], w_gate=w_gate[l],
                 b_gate=b_gate[l], w_br_ssm=w_br_ssm[l], w_br_pool=w_br_pool[l], w_br_mem=w_br_mem[l],
                 w_out=w_out[l], g_post1=g_post1[l], g_pre2=g_pre2[l], w_ffn_up=w_ffn_up[l],
                 w_dw=w_dw[l], b_dw=b_dw[l], w_ffn_down=w_ffn_down[l], g_post2=g_post2[l])
        mk_p, mv_p = memory_kv(mem_prompt, g_mem[l], w_mem_k[l], w_mem_v[l])
        z_ssm = jnp.zeros((BATCH, N_SSM_GROUPS, SSM_STATE), dt)
        z_pool = jnp.zeros((BATCH, POOL_BUF, D_POOL), dt)
        z_conv = jnp.zeros((BATCH, CONV_W - 1, D_FF), dt)
        xp, sr_p, si_p, pb_p, cb_p = layer(xp, mk_p, mv_p, z_ssm, z_ssm, z_pool, z_conv, 0, p)
        xs, sr_s, si_s, pb_s, cb_s = layer(xs, cache_mem_k[l], cache_mem_v[l], state_ssm_re[l],
                                           state_ssm_im[l], state_pool[l], state_conv[l], PAST_LEN, p)
        mk_l.append(mk_p)
        mv_l.append(mv_p)
        srp_l.append(sr_p)
        sip_l.append(si_p)
        srs_l.append(sr_s)
        sis_l.append(si_s)
        pp_l.append(pb_p)
        ps_l.append(pb_s)
        cp_l.append(cb_p)
        cs_l.append(cb_s)
    return (xp, xs, jnp.stack(mk_l), jnp.stack(mv_l), jnp.stack(srp_l), jnp.stack(sip_l),
            jnp.stack(srs_l), jnp.stack(sis_l), jnp.stack(pp_l), jnp.stack(ps_l),
            jnp.stack(cp_l), jnp.stack(cs_l))
```

```python
import functools
import math

import jax
import jax.numpy as jnp
from jax import lax
from jax.experimental import pallas as pl
from jax.experimental.pallas import tpu as pltpu

F32 = jnp.float32
BF16 = jnp.bfloat16

D_MODEL = 1024
D_SSM = 512
SSM_GROUP = 16
N_SSM_GROUPS = 32
SSM_STATE = 64
D_POOL = 256
POOL_WINDOWS = (2, 4, 8, 16)
POOL_GROUP = 64
POOL_BUF = 15
N_MEM = 256
MEM_HEADS = 4
D_MEM = 256
MEM_HEAD_DIM = 64
N_BRANCH = 3
D_FF = 2816
CONV_W = 3
EPS = 1e-6
PAST_LEN = 2048

SSM_BLK_IN = 256
SSM_BLOCKS = D_SSM // SSM_BLK_IN
SSM_BLK_STATES = (SSM_BLK_IN // SSM_GROUP) * SSM_STATE
SSM_COLS = 2 * SSM_BLK_STATES * SSM_BLOCKS
CUMSUM_ROWS = 256

POOL_HIST = 16
CONV_HIST = 8
STATE_ROWS = 8
FFN_CHUNKS = 2
FFN_CW = D_FF // FFN_CHUNKS

VMEM_LIMIT_BYTES = 60 * 1024 * 1024


def _dot(a, b):
    return jnp.dot(a, b, preferred_element_type=F32)


def _div_pow2(x, d):
    assert d & (d - 1) == 0
    return x >> (d.bit_length() - 1)


def _rmsnorm(x, g):
    ms = jnp.mean(x * x, axis=-1, keepdims=True)
    return x * lax.rsqrt(ms + EPS) * g


def _write_block_diag(k, v, kbd_ref, vbd_ref):
    kt = jnp.concatenate([k.T] * MEM_HEADS, axis=1)
    r = _div_pow2(lax.broadcasted_iota(jnp.int32, kt.shape, 0), MEM_HEAD_DIM)
    c = _div_pow2(lax.broadcasted_iota(jnp.int32, kt.shape, 1), N_MEM)
    kbd_ref[0] = jnp.where(r == c, kt, 0.0).astype(BF16)
    vt = jnp.concatenate([v] * MEM_HEADS, axis=0)
    r2 = _div_pow2(lax.broadcasted_iota(jnp.int32, vt.shape, 0), N_MEM)
    c2 = _div_pow2(lax.broadcasted_iota(jnp.int32, vt.shape, 1), MEM_HEAD_DIM)
    vbd_ref[0] = jnp.where(r2 == c2, vt, 0.0).astype(BF16)


def _memkv_kernel(mem_ref, g_ref, wk_ref, wv_ref, k_ref, v_ref, kbd_ref, vbd_ref):
    mn = _rmsnorm(mem_ref[0], g_ref[...]).astype(BF16)
    k = _dot(mn, wk_ref[...])
    v = _dot(mn, wv_ref[...])
    k_ref[0] = k
    v_ref[0] = v
    _write_block_diag(k, v, kbd_ref, vbd_ref)


def _blockdiag_kernel(k_ref, v_ref, kbd_ref, vbd_ref):
    _write_block_diag(k_ref[0], v_ref[0], kbd_ref, vbd_ref)


def _const_spec(shape):
    zeros = (0,) * len(shape)
    return pl.BlockSpec(shape, lambda *_: zeros, pipeline_mode=pl.Buffered(1))


def _bd_out(nb):
    shapes = (jax.ShapeDtypeStruct((nb, D_MEM, MEM_HEADS * N_MEM), BF16),
              jax.ShapeDtypeStruct((nb, MEM_HEADS * N_MEM, D_MEM), BF16))
    specs = (pl.BlockSpec((1, D_MEM, MEM_HEADS * N_MEM), lambda b: (b, 0, 0)),
             pl.BlockSpec((1, MEM_HEADS * N_MEM, D_MEM), lambda b: (b, 0, 0)))
    return shapes, specs


def _memory_kv(mem, g_mem, wk, wv):
    nb = mem.shape[0]
    bd_shapes, bd_specs = _bd_out(nb)
    kv_spec = pl.BlockSpec((1, N_MEM, D_MEM), lambda b: (b, 0, 0))
    return pl.pallas_call(
        _memkv_kernel,
        grid=(nb,),
        in_specs=[pl.BlockSpec((1, N_MEM, D_MODEL), lambda b: (b, 0, 0)),
                  _const_spec((1, D_MODEL)), _const_spec((D_MODEL, D_MEM)), _const_spec((D_MODEL, D_MEM))],
        out_specs=(kv_spec, kv_spec) + bd_specs,
        out_shape=(jax.ShapeDtypeStruct((nb, N_MEM, D_MEM), F32),) * 2 + bd_shapes,
        compiler_params=pltpu.CompilerParams(dimension_semantics=("arbitrary",)),
        name="memory_kv",
    )(mem, g_mem, wk, wv)


def _memory_blockdiag(k, v):
    nb = k.shape[0]
    bd_shapes, bd_specs = _bd_out(nb)
    kv_spec = pl.BlockSpec((1, N_MEM, D_MEM), lambda b: (b, 0, 0))
    return pl.pallas_call(
        _blockdiag_kernel,
        grid=(nb,),
        in_specs=[kv_spec, kv_spec],
        out_specs=bd_specs,
        out_shape=bd_shapes,
        compiler_params=pltpu.CompilerParams(dimension_semantics=("arbitrary",)),
        name="memory_blockdiag",
    )(k, v)


def _mixer_kernel(n_seq, seq_len, tc, has_state, pos0, *refs):
    refs = list(refs)
    x_ref, kbd_ref, vbd_ref = refs[:3]
    del refs[:3]
    if has_state:
        h0_ref, pool0_ref = refs[:2]
        del refs[:2]
    (g1_ref, win_ref, bmat_ref, cmat_ref, tneg_ref, tpos_ref, acar_ref, ltri_ref, d_ref,
     wglu_ref, bglu_ref, wpool_ref, pscale_ref, wgate_ref, bgate_ref, wbs_ref, wbp_ref, wbm_ref,
     wout_ref, g2_ref,
     x1_ref, hlast_ref, ptail_ref,
     hprev_ref, pext_ref, v_scr, h_scr) = refs

    rows = n_seq * seq_len
    n_sub = seq_len // tc
    half = SSM_BLK_STATES
    i = pl.program_id(1)

    @pl.when(i == 0)
    def _init():
        if has_state:
            for k in range(2 * SSM_BLOCKS):
                hprev_ref[k] = h0_ref[:, k * half:(k + 1) * half]
            pext_ref[:, 0:POOL_HIST, :] = pool0_ref[...]
        else:
            hprev_ref[...] = jnp.zeros(hprev_ref.shape, F32)
            pext_ref[:, 0:POOL_HIST, :] = jnp.zeros((n_seq, POOL_HIST, D_POOL), F32)

    x = x_ref[...].reshape(rows, D_MODEL)
    hb = _rmsnorm(x, g1_ref[...]).astype(BF16)
    z = _dot(hb, win_ref[...])
    u_ssm = z[:, :D_SSM]
    u_pool = z[:, D_SSM:D_SSM + D_POOL]
    q_mem = z[:, D_SSM + D_POOL:]

    ltri = ltri_ref[...]
    y_blocks = []
    for blk in range(SSM_BLOCKS):
        c0 = 2 * half * blk
        ub = u_ssm[:, SSM_BLK_IN * blk:SSM_BLK_IN * (blk + 1)].astype(BF16)
        bu = _dot(ub, bmat_ref[blk])
        n_re = tneg_ref[:, c0:c0 + half]
        n_im = tneg_ref[:, c0 + half:c0 + 2 * half]
        for c in range(rows // tc):
            b_re = bu[c * tc:(c + 1) * tc, :half]
            b_im = bu[c * tc:(c + 1) * tc, half:]
            v_scr[blk, c * tc:(c + 1) * tc, 0:half] = (b_re * n_re - b_im * n_im).astype(BF16)
            v_scr[blk, c * tc:(c + 1) * tc, half:2 * half] = (b_re * n_im + b_im * n_re).astype(BF16)
        ws = [_dot(ltri, v_scr[blk, g * CUMSUM_ROWS:(g + 1) * CUMSUM_ROWS, :])
              for g in range(rows // CUMSUM_ROWS)]
        p_re = tpos_ref[:, c0:c0 + half]
        p_im = tpos_ref[:, c0 + half:c0 + 2 * half]
        a_re = acar_ref[:, c0:c0 + half]
        a_im = acar_ref[:, c0 + half:c0 + 2 * half]
        for s in range(n_seq):
            hp_re = hprev_ref[2 * blk, s:s + 1, :]
            hp_im = hprev_ref[2 * blk + 1, s:s + 1, :]
            for j in range(n_sub):
                r0 = s * seq_len + j * tc
                wblk = ws[r0 // CUMSUM_ROWS][r0 % CUMSUM_ROWS:r0 % CUMSUM_ROWS + tc]
                g_re = wblk[:, :half] + (a_re * hp_re - a_im * hp_im)
                g_im = wblk[:, half:] + (a_re * hp_im + a_im * hp_re)
                h_re = p_re * g_re - p_im * g_im
                h_im = p_re * g_im + p_im * g_re
                h_scr[blk, r0:r0 + tc, 0:half] = h_re.astype(BF16)
                h_scr[blk, r0:r0 + tc, half:2 * half] = h_im.astype(BF16)
                hp_re = h_re[tc - 1:tc, :]
                hp_im = h_im[tc - 1:tc, :]
            hprev_ref[2 * blk, s:s + 1, :] = hp_re
            hprev_ref[2 * blk + 1, s:s + 1, :] = hp_im
        y_blocks.append(_dot(h_scr[blk], cmat_ref[blk]))
    y = jnp.concatenate(y_blocks, axis=1) + d_ref[...] * u_ssm
    y = jax.nn.gelu(y)
    y_ssm = y * jax.nn.sigmoid(_dot(y.astype(BF16), wglu_ref[...]) + bglu_ref[...])

    for k in range(2 * SSM_BLOCKS):
        for s in range(n_seq):
            hlast_ref[s, :, k * half:(k + 1) * half] = jnp.broadcast_to(
                hprev_ref[k, s:s + 1, :], (STATE_ROWS, half))

    up3 = u_pool.reshape(n_seq, seq_len, D_POOL)
    pext_ref[:, POOL_HIST:POOL_HIST + seq_len, :] = up3
    acc = up3
    win_sums = {}
    for k in range(1, max(POOL_WINDOWS)):
        acc = acc + pext_ref[:, POOL_HIST - k:POOL_HIST - k + seq_len, :]
        if k + 1 in POOL_WINDOWS:
            win_sums[k + 1] = acc
    shape3 = (n_seq, seq_len, D_POOL)
    pos = pos0 + i * seq_len + lax.broadcasted_iota(jnp.int32, shape3, 1)
    grp = _div_pow2(lax.broadcasted_iota(jnp.int32, shape3, 2), POOL_GROUP)
    sel = win_sums[POOL_WINDOWS[-1]]
    wlane = jnp.full(shape3, POOL_WINDOWS[-1], jnp.int32)
    for gi in range(len(POOL_WINDOWS) - 2, -1, -1):
        sel = jnp.where(grp == gi, win_sums[POOL_WINDOWS[gi]], sel)
        wlane = jnp.where(grp == gi, POOL_WINDOWS[gi], wlane)
    cnt = jnp.minimum(pos + 1, wlane).astype(F32)
    diff = (sel / cnt - up3).reshape(rows, D_POOL)
    y_pool = _dot(diff.astype(BF16), wpool_ref[...]) * pscale_ref[...]
    ptail_ref[...] = up3[:, seq_len - POOL_HIST:, :]
    pext_ref[:, 0:POOL_HIST, :] = up3[:, seq_len - POOL_HIST:, :]

    qb = q_mem.astype(BF16)
    scale = MEM_HEAD_DIM ** -0.5
    outs = []
    for s in range(n_seq):
        sc = _dot(qb[s * seq_len:(s + 1) * seq_len], kbd_ref[s]) * scale
        probs = []
        for hd in range(MEM_HEADS):
            sh = sc[:, hd * N_MEM:(hd + 1) * N_MEM]
            e = jnp.exp(sh - jnp.max(sh, axis=-1, keepdims=True))
            probs.append(e / jnp.sum(e, axis=-1, keepdims=True))
        p = jnp.concatenate(probs, axis=1).astype(BF16)
        outs.append(_dot(p, vbd_ref[s]))
    y_mem = outs[0] if n_seq == 1 else jnp.concatenate(outs, axis=0)

    def gate(k):
        cols = slice(k * D_MODEL, (k + 1) * D_MODEL)
        return jax.nn.sigmoid(_dot(hb, wgate_ref[:, cols]) + bgate_ref[:, cols])

    merged = gate(0) * _dot(y_ssm.astype(BF16), wbs_ref[...])
    merged = merged + gate(1) * _dot(y_pool.astype(BF16), wbp_ref[...])
    merged = merged + gate(2) * _dot(y_mem.astype(BF16), wbm_ref[...])
    mo = _dot(merged.astype(BF16), wout_ref[...])
    x1 = x + _rmsnorm(mo, g2_ref[...])
    x1_ref[...] = x1.reshape(n_seq, seq_len, D_MODEL)


def _mixer(x, kbd, vbd, state, consts, *, n_seq, seq_len, tc, pos0):
    nb, total_len, _ = x.shape
    has_state = state is not None
    grid = (nb // n_seq, total_len // seq_len)
    rows = n_seq * seq_len
    assert rows % CUMSUM_ROWS == 0 and seq_len % tc == 0 and CUMSUM_ROWS % tc == 0
    assert seq_len >= POOL_HIST and tc % 16 == 0

    in_specs = [pl.BlockSpec((n_seq, seq_len, D_MODEL), lambda b, i: (b, i, 0)),
                pl.BlockSpec((n_seq, D_MEM, MEM_HEADS * N_MEM), lambda b, i: (b, 0, 0)),
                pl.BlockSpec((n_seq, MEM_HEADS * N_MEM, D_MEM), lambda b, i: (b, 0, 0))]
    args = [x, kbd, vbd]
    if has_state:
        assert grid == (1, 1)
        in_specs += [_const_spec(a.shape) for a in state]
        args += list(state)
    in_specs += [_const_spec(a.shape) for a in consts]
    args += list(consts)

    out_shape = (jax.ShapeDtypeStruct(x.shape, F32),
                 jax.ShapeDtypeStruct((nb, STATE_ROWS, SSM_COLS), F32),
                 jax.ShapeDtypeStruct((nb, POOL_HIST, D_POOL), F32))
    out_specs = (pl.BlockSpec((n_seq, seq_len, D_MODEL), lambda b, i: (b, i, 0)),
                 pl.BlockSpec((n_seq, STATE_ROWS, SSM_COLS), lambda b, i: (b, 0, 0)),
                 pl.BlockSpec((n_seq, POOL_HIST, D_POOL), lambda b, i: (b, 0, 0)))
    scratch = [pltpu.VMEM((2 * SSM_BLOCKS, n_seq, SSM_BLK_STATES), F32),
               pltpu.VMEM((n_seq, POOL_HIST + seq_len, D_POOL), F32),
               pltpu.VMEM((SSM_BLOCKS, rows, 2 * SSM_BLK_STATES), BF16),
               pltpu.VMEM((SSM_BLOCKS, rows, 2 * SSM_BLK_STATES), BF16)]
    return pl.pallas_call(
        functools.partial(_mixer_kernel, n_seq, seq_len, tc, has_state, pos0),
        grid=grid,
        in_specs=in_specs,
        out_specs=out_specs,
        out_shape=out_shape,
        scratch_shapes=scratch,
        compiler_params=pltpu.CompilerParams(dimension_semantics=("arbitrary", "arbitrary"),
                                             vmem_limit_bytes=VMEM_LIMIT_BYTES),
        name="mixer_state" if has_state else "mixer",
    )(*args)


def _ffn_kernel(n_seq, seq_len, has_state, *refs):
    refs = list(refs)
    x_ref = refs.pop(0)
    if has_state:
        conv0_ref = refs.pop(0)
    (g1_ref, wup_ref, wdw_ref, bdw_ref, wdown_ref, g2_ref,
     out_ref, ctail_ref,
     hist_ref, ext_ref) = refs

    rows = n_seq * seq_len
    i = pl.program_id(1)

    @pl.when(i == 0)
    def _init():
        if has_state:
            hist_ref[...] = conv0_ref[...]
        else:
            hist_ref[...] = jnp.zeros(hist_ref.shape, F32)

    x = x_ref[...].reshape(rows, D_MODEL)
    hb = _rmsnorm(x, g1_ref[...]).astype(BF16)
    acc = None
    for j in range(FFN_CHUNKS):
        cols = slice(j * FFN_CW, (j + 1) * FFN_CW)
        gcols = slice(D_FF + j * FFN_CW, D_FF + (j + 1) * FFN_CW)
        a3 = _dot(hb, wup_ref[:, cols]).reshape(n_seq, seq_len, FFN_CW)
        gate = _dot(hb, wup_ref[:, gcols])
        ext_ref[:, 0:CONV_HIST, :] = hist_ref[:, :, cols]
        ext_ref[:, CONV_HIST:CONV_HIST + seq_len, :] = a3
        hist_ref[:, :, cols] = a3[:, seq_len - CONV_HIST:, :]
        conv = ext_ref[:, CONV_HIST - 2:CONV_HIST - 2 + seq_len, :] * wdw_ref[0:1, cols]
        conv = conv + ext_ref[:, CONV_HIST - 1:CONV_HIST - 1 + seq_len, :] * wdw_ref[1:2, cols]
        conv = conv + a3 * wdw_ref[2:3, cols]
        conv = conv + bdw_ref[:, cols]
        act = jax.nn.gelu(conv).reshape(rows, FFN_CW) * gate
        part = _dot(act.astype(BF16), wdown_ref[cols, :])
        acc = part if acc is None else acc + part
    out = x + _rmsnorm(acc, g2_ref[...])
    out_ref[...] = out.reshape(n_seq, seq_len, D_MODEL)
    ctail_ref[...] = hist_ref[...]


def _conv_ffn(x, conv0, consts, *, n_seq, seq_len):
    nb, total_len, _ = x.shape
    has_state = conv0 is not None
    grid = (nb // n_seq, total_len // seq_len)
    assert seq_len >= CONV_HIST and CONV_W - 1 <= CONV_HIST

    in_specs = [pl.BlockSpec((n_seq, seq_len, D_MODEL), lambda b, i: (b, i, 0))]
    args = [x]
    if has_state:
        assert grid == (1, 1)
        in_specs.append(_const_spec(conv0.shape))
        args.append(conv0)
    in_specs += [_const_spec(a.shape) for a in consts]
    args += list(consts)

    out_shape = (jax.ShapeDtypeStruct(x.shape, F32),
                 jax.ShapeDtypeStruct((nb, CONV_HIST, D_FF), F32))
    out_specs = (pl.BlockSpec((n_seq, seq_len, D_MODEL), lambda b, i: (b, i, 0)),
                 pl.BlockSpec((n_seq, CONV_HIST, D_FF), lambda b, i: (b, 0, 0)))
    scratch = [pltpu.VMEM((n_seq, CONV_HIST, D_FF), F32),
               pltpu.VMEM((n_seq, CONV_HIST + seq_len, FFN_CW), F32)]
    return pl.pallas_call(
        functools.partial(_ffn_kernel, n_seq, seq_len, has_state),
        grid=grid,
        in_specs=in_specs,
        out_specs=out_specs,
        out_shape=out_shape,
        scratch_shapes=scratch,
        compiler_params=pltpu.CompilerParams(dimension_semantics=("arbitrary", "arbitrary"),
                                             vmem_limit_bytes=VMEM_LIMIT_BYTES),
        name="conv_ffn_state" if has_state else "conv_ffn",
    )(*args)


def _ssm_columns(t):
    return t.reshape(t.shape[:-2] + (SSM_BLOCKS, SSM_BLK_STATES))


def _interleave_re_im(re, im):
    st = jnp.stack([re, im], axis=-2)
    return st.reshape(st.shape[:-3] + (SSM_COLS,))


def _ssm_tables(lam_re, lam_im, log_dt, tc):
    dt = jnp.exp(log_dt)[:, None]
    c = tc // 2

    def power(k):
        mag = jnp.exp(k * (lam_re * dt))
        ang = k * (lam_im * dt)
        return _interleave_re_im(_ssm_columns(mag * jnp.cos(ang)), _ssm_columns(mag * jnp.sin(ang)))

    steps = jnp.arange(tc, dtype=F32)[:, None, None]
    return power(c - steps), power(steps - c), power(jnp.full((1, 1, 1), c + 1.0, F32))


def _ssm_matrices(lam_re, lam_im, log_dt, b_re, b_im, c_re, c_im):
    dt = jnp.exp(log_dt)[:, None]
    mag = jnp.exp(lam_re * dt)
    ab_re = mag * jnp.cos(lam_im * dt)
    ab_im = mag * jnp.sin(lam_im * dt)
    den = lam_re * lam_re + lam_im * lam_im
    nr = ab_re - 1.0
    ni = ab_im
    z_re = (nr * lam_re + ni * lam_im) / den
    z_im = (ni * lam_re - nr * lam_im) / den
    bb_re = z_re[..., None] * b_re - z_im[..., None] * b_im
    bb_im = z_re[..., None] * b_im + z_im[..., None] * b_re
    gpb = SSM_BLK_IN // SSM_GROUP
    eye = jnp.eye(gpb, dtype=F32)

    def in_block(bb):
        t = bb.transpose(0, 2, 1).reshape(SSM_BLOCKS, gpb, SSM_GROUP, SSM_STATE)
        return jnp.einsum('bghn,gk->bghkn', t, eye).reshape(SSM_BLOCKS, SSM_BLK_IN, SSM_BLK_STATES)

    def out_block(cc):
        t = cc.transpose(0, 2, 1).reshape(SSM_BLOCKS, gpb, SSM_STATE, SSM_GROUP)
        return jnp.einsum('bgnh,gk->bgnkh', t, eye).reshape(SSM_BLOCKS, SSM_BLK_STATES, SSM_BLK_IN)

    bmat = jnp.concatenate([in_block(bb_re), in_block(bb_im)], axis=2).astype(BF16)
    cmat = jnp.concatenate([out_block(c_re), -out_block(c_im)], axis=1).astype(BF16)
    return bmat, cmat


def _cumsum_matrix(tc):
    tri = jnp.tril(jnp.ones((tc, tc), F32))
    return jnp.kron(jnp.eye(CUMSUM_ROWS // tc, dtype=F32), tri).astype(BF16)


def _pool_matrix(w_pool):
    eye = jnp.eye(len(POOL_WINDOWS), dtype=F32)
    return jnp.einsum('gcd,gk->gckd', w_pool, eye).reshape(D_POOL, D_POOL).astype(BF16)


def _row(v):
    return v.reshape(1, -1)


def _state_to_cols(re, im):
    return _interleave_re_im(_ssm_columns(re), _ssm_columns(im))


def _cols_to_state(cols):
    t = cols.reshape(cols.shape[0], SSM_BLOCKS, 2, SSM_BLK_STATES)
    shape = (cols.shape[0], N_SSM_GROUPS, SSM_STATE)
    return t[:, :, 0].reshape(shape), t[:, :, 1].reshape(shape)


PROMPT_TILE = 512
PROMPT_TC = 64


def kernel(x_prompt, x_sample, mem_prompt, cache_mem_k, cache_mem_v, state_ssm_re, state_ssm_im, state_pool, state_conv, g_pre1, w_in, ssm_lam_re, ssm_lam_im, ssm_log_dt, ssm_b_re, ssm_b_im, ssm_c_re, ssm_c_im, ssm_d, w_glu, b_glu, w_pool, pool_scale, g_mem, w_mem_k, w_mem_v, w_gate, b_gate, w_br_ssm, w_br_pool, w_br_mem, w_out, g_post1, g_pre2, w_ffn_up, w_dw, b_dw, w_ffn_down, g_post2):
    assert g_pre1.shape[0] == 1
    dec_batch, dec_seq, _ = x_sample.shape

    bmat, cmat = _ssm_matrices(ssm_lam_re[0], ssm_lam_im[0], ssm_log_dt[0], ssm_b_re[0], ssm_b_im[0],
                               ssm_c_re[0], ssm_c_im[0])

    def mixer_consts(tc):
        tneg, tpos, acar = _ssm_tables(ssm_lam_re[0], ssm_lam_im[0], ssm_log_dt[0], tc)
        return (g_pre1, w_in[0].astype(BF16), bmat, cmat, tneg, tpos, acar, _cumsum_matrix(tc),
                ssm_d, w_glu[0].astype(BF16), b_glu, _pool_matrix(w_pool[0]), pool_scale,
                w_gate[0].astype(BF16), b_gate, w_br_ssm[0].astype(BF16), w_br_pool[0].astype(BF16),
                w_br_mem[0].astype(BF16), w_out[0].astype(BF16), g_post1)

    ffn_consts = (g_pre2, w_ffn_up[0].astype(BF16), w_dw[0], b_dw, w_ffn_down[0].astype(BF16), g_post2)

    mk_p, mv_p, kbd_p, vbd_p = _memory_kv(mem_prompt, g_mem, w_mem_k[0].astype(BF16), w_mem_v[0].astype(BF16))
    x1_p, hlast_p, ptail_p = _mixer(x_prompt, kbd_p, vbd_p, None, mixer_consts(PROMPT_TC),
                                    n_seq=1, seq_len=PROMPT_TILE, tc=PROMPT_TC, pos0=0)
    y_p, ctail_p = _conv_ffn(x1_p, None, ffn_consts, n_seq=1, seq_len=PROMPT_TILE)

    kbd_s, vbd_s = _memory_blockdiag(cache_mem_k[0].reshape(dec_batch, N_MEM, D_MEM),
                                     cache_mem_v[0].reshape(dec_batch, N_MEM, D_MEM))
    h0 = _state_to_cols(state_ssm_re[0], state_ssm_im[0])
    pool0 = jnp.pad(state_pool[0], ((0, 0), (POOL_HIST - POOL_BUF, 0), (0, 0)))
    conv0 = jnp.pad(state_conv[0], ((0, 0), (CONV_HIST - (CONV_W - 1), 0), (0, 0)))
    x1_s, hlast_s, ptail_s = _mixer(x_sample, kbd_s, vbd_s, (h0, pool0), mixer_consts(dec_seq),
                                    n_seq=dec_batch, seq_len=dec_seq, tc=dec_seq, pos0=PAST_LEN)
    y_s, ctail_s = _conv_ffn(x1_s, conv0, ffn_consts, n_seq=dec_batch, seq_len=dec_seq)

    heads = (MEM_HEADS, MEM_HEAD_DIM)
    sr_p, si_p = _cols_to_state(hlast_p[:, 0])
    sr_s, si_s = _cols_to_state(hlast_s[:, 0])
    return (y_p, y_s,
            mk_p.reshape(mk_p.shape[:2] + heads)[None], mv_p.reshape(mv_p.shape[:2] + heads)[None],
            sr_p[None], si_p[None], sr_s[None], si_s[None],
            ptail_p[:, POOL_HIST - POOL_BUF:][None], ptail_s[:, POOL_HIST - POOL_BUF:][None],
            ctail_p[:, CONV_HIST - (CONV_W - 1):][None], ctail_s[:, CONV_HIST - (CONV_W - 1):][None])
```
